```python
import math
import jax, jax.numpy as jnp
from jax import lax
import numpy as np

D_MODEL = 2048
BATCH = 8
SEQ = 2048
DEPTH = 2
DEC_BATCH = 2
DEC_SEQ = 4096
PAST_LEN = 128

N_MIXERS = 2
N_POOL_LAYERS = (DEPTH + 1) // 2
N_HYENA_LAYERS = DEPTH // 2
POOL_WINDOWS = (2, 4, 8, 16)
N_POOL_GROUPS = 4
POOL_GROUP = D_MODEL // N_POOL_GROUPS
HYENA_PROJ = 3
FILTER_EMB = 33
FILTER_BANDS = (FILTER_EMB - 1) // 2
FILTER_HIDDEN = 64
N_DIRS = 2
DECAY_TARGET = 1e-2
FAST_DECAY_PCT = 0.3
SLOW_DECAY_PCT = 1.5
PEER_HEADS = 8
PEER_N_KEYS = 128
PEER_N_EXPERTS = PEER_N_KEYS * PEER_N_KEYS
PEER_TOPK = 16
PEER_DKEY = 256
PEER_HALF = PEER_DKEY // 2
TOKEN_BLOCK = 128
RMS_EPS = 1e-6

kernel_name = 'hybrid_pool_hyena_peer_encoder'


def _rmsnorm(x, g):
    xf = x.astype(jnp.float32)
    y = xf * lax.rsqrt(jnp.mean(xf * xf, axis=-1, keepdims=True) + RMS_EPS)
    return (y * g.astype(jnp.float32)).astype(x.dtype)


def _pool_mixer(x, w, scale):
    B, L, D = x.shape
    xf = x.astype(jnp.float32)
    cs = jnp.pad(jnp.cumsum(xf, axis=1), ((0, 0), (1, 0), (0, 0)))
    t = jnp.arange(L)
    means = []
    for g, win in enumerate(POOL_WINDOWS):
        lo = jnp.clip(t - win // 2, 0, L - 1)
        hi = jnp.clip(t + win - 1 - win // 2, 0, L - 1)
        csg = cs[:, :, g * POOL_GROUP:(g + 1) * POOL_GROUP]
        s = jnp.take(csg, hi + 1, axis=1) - jnp.take(csg, lo, axis=1)
        cnt = (hi - lo + 1).astype(jnp.float32)[None, :, None]
        means.append(s / cnt)
    mean = jnp.stack(means, axis=2)
    diff = (mean - xf.reshape(B, L, N_POOL_GROUPS, POOL_GROUP)).astype(x.dtype)
    y = jnp.einsum('blgc,gcd->bld', diff, w)
    return y * scale


def _hyena_filter(L, D, w1, b1, fr1, w2, b2, fr2, w3):
    f32 = jnp.float32
    t = jnp.linspace(0.0, 1.0, L, dtype=f32)[:, None]
    ang_base = 2.0 * math.pi * jnp.arange(L, dtype=f32) / L
    bands = jnp.linspace(1e-4, FILTER_BANDS - 1, FILTER_BANDS, dtype=f32)
    ang = ang_base[:, None] * bands[None, :]
    z = jnp.concatenate([t, jnp.cos(ang), -jnp.sin(ang)], axis=-1)
    h = jnp.sin(fr1.astype(f32) * (z @ w1.astype(f32) + b1.astype(f32)))
    h = jnp.sin(fr2.astype(f32) * (h @ w2.astype(f32) + b2.astype(f32)))
    k = (h @ w3.astype(f32)).reshape(L, N_DIRS, D)
    max_decay = math.log(DECAY_TARGET) / FAST_DECAY_PCT
    min_decay = math.log(DECAY_TARGET) / SLOW_DECAY_PCT
    deltas = jnp.abs(jnp.linspace(min_decay, max_decay, D, dtype=f32))
    k = k * jnp.exp(-t * deltas[None, :])[:, None, :]
    k_f, k_b = k[:, 0], k[:, 1]
    two = jnp.concatenate([k_f[:1] + k_b[:1], k_f[1:], jnp.zeros((1, D), f32), k_b[:0:-1]], axis=0)
    return two / jnp.sum(jnp.abs(two), axis=0, keepdims=True)


def _hyena_mixer(x, w_in, b_in, w_short, f_w1, f_b1, f_fr1, f_w2, f_b2, f_fr2, f_w3, skip, w_out, b_out):
    B, L, D = x.shape
    u = x @ w_in + b_in
    up = jnp.pad(u, ((0, 0), (1, 1), (0, 0)))
    u = up[:, :-2] * w_short[0] + up[:, 1:-1] * w_short[1] + up[:, 2:] * w_short[2]
    x0, x1, v = jnp.split(u, HYENA_PROJ, axis=-1)
    zf = (v * x1).astype(jnp.float32)
    k = _hyena_filter(L, D, f_w1, f_b1, f_fr1, f_w2, f_b2, f_fr2, f_w3)
    Z = jnp.fft.rfft(zf, n=2 * L, axis=1)
    K = jnp.fft.rfft(k, n=2 * L, axis=0)
    conv = jnp.fft.irfft(Z * K[None], n=2 * L, axis=1)[:, :L]
    y = (conv + zf * skip.astype(jnp.float32)).astype(x.dtype) * x0
    return y @ w_out + b_out


def _peer(x, w_q, sub_keys, u, v):
    B, L, D = x.shape
    xt = x.reshape(-1, D)
    T = xt.shape[0]
    q = (xt @ w_q).reshape(T, PEER_HEADS, 2, PEER_HALF)
    s = jnp.einsum('thpc,hpnc->thpn', q, sub_keys).astype(jnp.float32)
    sv, si = lax.top_k(s, PEER_TOPK)
    cand_s = (sv[:, :, 0, :, None] + sv[:, :, 1, None, :]).reshape(T, PEER_HEADS, PEER_TOPK * PEER_TOPK)
    cand_i = (si[:, :, 0, :, None] * PEER_N_KEYS + si[:, :, 1, None, :]).reshape(T, PEER_HEADS, PEER_TOPK * PEER_TOPK)
    top_s, pos = lax.top_k(cand_s, PEER_TOPK)
    experts = jnp.take_along_axis(cand_i, pos, axis=-1)
    gates = jax.nn.softmax(top_s, axis=-1).astype(x.dtype)
    nb = T // TOKEN_BLOCK

    def block(args):
        xb, eb, gb = args
        ub = jnp.take(u, eb, axis=0)
        vb = jnp.take(v, eb, axis=0)
        a = jax.nn.gelu(jnp.einsum('td,thkd->thk', xb, ub), approximate=False) * gb
        return jnp.einsum('thk,thkd->td', a, vb)

    out = lax.map(block, (xt.reshape(nb, TOKEN_BLOCK, D),
                          experts.reshape(nb, TOKEN_BLOCK, PEER_HEADS, PEER_TOPK),
                          gates.reshape(nb, TOKEN_BLOCK, PEER_HEADS, PEER_TOPK)))
    return out.reshape(B, L, D)


def _trunk(x, mix_gain, ffn_gain, final_gain, pool_w, pool_scale,
           hy_w_in, hy_b_in, hy_short, hy_f_w1, hy_f_b1, hy_f_freq1, hy_f_w2, hy_f_b2, hy_f_freq2, hy_f_w3,
           hy_skip, hy_w_out, hy_b_out, peer_w_q, peer_sub_keys, peer_u, peer_v):
    for i in range(DEPTH):
        h = _rmsnorm(x, mix_gain[i])
        j = i // N_MIXERS
        if i % N_MIXERS == 0:
            x = x + _pool_mixer(h, pool_w[j], pool_scale[j])
        else:
            x = x + _hyena_mixer(h, hy_w_in[j], hy_b_in[j], hy_short[j], hy_f_w1[j], hy_f_b1[j], hy_f_freq1[j],
                                 hy_f_w2[j], hy_f_b2[j], hy_f_freq2[j], hy_f_w3[j], hy_skip[j], hy_w_out[j], hy_b_out[j])
        x = x + _peer(_rmsnorm(x, ffn_gain[i]), peer_w_q[i], peer_sub_keys[i], peer_u[i], peer_v[i])
    return _rmsnorm(x, final_gain)


def setup_inputs(seed: int = 0) -> dict:
    key = jax.random.key(seed)
    ks = iter(jax.random.split(key, 32))
    D = D_MODEL

    def nrm(shape, scale):
        return jax.random.normal(next(ks), shape, jnp.float32) * scale

    return {
        'x_prompt': nrm((BATCH, SEQ, D), 1.0),
        'x_sample': nrm((DEC_BATCH, DEC_SEQ, D), 1.0),
        'mix_gain': 1.0 + nrm((DEPTH, D), 0.02),
        'ffn_gain': 1.0 + nrm((DEPTH, D), 0.02),
        'final_gain': 1.0 + nrm((D,), 0.02),
        'pool_w': nrm((N_POOL_LAYERS, N_POOL_GROUPS, POOL_GROUP, D), POOL_GROUP ** -0.5),
        'pool_scale': 0.5 + nrm((N_POOL_LAYERS, D), 0.05),
        'hy_w_in': nrm((N_HYENA_LAYERS, D, HYENA_PROJ * D), D ** -0.5),
        'hy_b_in': nrm((N_HYENA_LAYERS, HYENA_PROJ * D), 0.02),
        'hy_short': nrm((N_HYENA_LAYERS, 3, HYENA_PROJ * D), 3 ** -0.5),
        'hy_f_w1': nrm((N_HYENA_LAYERS, FILTER_EMB, FILTER_HIDDEN), FILTER_EMB ** -0.5),
        'hy_f_b1': nrm((N_HYENA_LAYERS, FILTER_HIDDEN), 0.1),
        'hy_f_freq1': 1.0 + nrm((N_HYENA_LAYERS, FILTER_HIDDEN), 0.05),
        'hy_f_w2': nrm((N_HYENA_LAYERS, FILTER_HIDDEN, FILTER_HIDDEN), FILTER_HIDDEN ** -0.5),
        'hy_f_b2': nrm((N_HYENA_LAYERS, FILTER_HIDDEN), 0.1),
        'hy_f_freq2': 1.0 + nrm((N_HYENA_LAYERS, FILTER_HIDDEN), 0.05),
        'hy_f_w3': nrm((N_HYENA_LAYERS, FILTER_HIDDEN, N_DIRS * D), FILTER_HIDDEN ** -0.5),
        'hy_skip': nrm((N_HYENA_LAYERS, D), 1.0),
        'hy_w_out': nrm((N_HYENA_LAYERS, D, D), D ** -0.5),
        'hy_b_out': nrm((N_HYENA_LAYERS, D), 0.02),
        'peer_w_q': nrm((DEPTH, D, PEER_HEADS * PEER_DKEY), D ** -0.5),
        'peer_sub_keys': nrm((DEPTH, PEER_HEADS, 2, PEER_N_KEYS, PEER_HALF), PEER_HALF ** -0.5),
        'peer_u': nrm((DEPTH, PEER_N_EXPERTS, D), D ** -0.5),
        'peer_v': nrm((DEPTH, PEER_N_EXPERTS, D), PEER_HEADS ** -0.5),
    }


def reference(x_prompt, x_sample, mix_gain, ffn_gain, final_gain, pool_w, pool_scale,
              hy_w_in, hy_b_in, hy_short, hy_f_w1, hy_f_b1, hy_f_freq1, hy_f_w2, hy_f_b2, hy_f_freq2, hy_f_w3,
              hy_skip, hy_w_out, hy_b_out, peer_w_q, peer_sub_keys, peer_u, peer_v):
    weights = (mix_gain, ffn_gain, final_gain, pool_w, pool_scale,
               hy_w_in, hy_b_in, hy_short, hy_f_w1, hy_f_b1, hy_f_freq1, hy_f_w2, hy_f_b2, hy_f_freq2, hy_f_w3,
               hy_skip, hy_w_out, hy_b_out, peer_w_q, peer_sub_keys, peer_u, peer_v)
    y_prompt = _trunk(x_prompt, *weights)
    y_sample = _trunk(x_sample, *weights)
    return (y_prompt, y_sample)
```

```python
import functools
import math

import numpy as np
import jax
import jax.numpy as jnp
from jax import lax
from jax.experimental import pallas as pl
from jax.experimental.pallas import tpu as pltpu

F32 = jnp.float32
BF16 = jnp.bfloat16

RMS_EPS = 1e-6
POOL_WINDOWS = (2, 4, 8, 16)
POOL_HALO = 8
PEER_TOPK = 16
FILTER_BANDS = 16
DECAY_TARGET = 1e-2
FAST_DECAY_PCT = 0.3
SLOW_DECAY_PCT = 1.5
INV_SQRT2 = 0.7071067811865476
NEG_INF = float("-inf")

DFT_R = 256
DFT_CHUNK = 16
VMEM_LIMIT = 56 * 1024 * 1024


def _cparams(sem):
    return pltpu.CompilerParams(dimension_semantics=sem, vmem_limit_bytes=VMEM_LIMIT)


def _rms_kernel(x_ref, g_ref, o_ref):
    x = x_ref[...]
    ms = jnp.mean(x * x, axis=-1, keepdims=True)
    o_ref[...] = (x * lax.rsqrt(ms + RMS_EPS) * g_ref[...]).astype(o_ref.dtype)


def _rmsnorm(x, g, out_dtype, *, row_start=0, n_rows=None, tile=256):
    T, D = x.shape
    n_rows = T if n_rows is None else n_rows
    off = row_start // tile
    return pl.pallas_call(
        _rms_kernel,
        out_shape=jax.ShapeDtypeStruct((n_rows, D), out_dtype),
        grid=(n_rows // tile,),
        in_specs=[pl.BlockSpec((tile, D), lambda i: (i + off, 0)),
                  pl.BlockSpec((1, D), lambda i: (0, 0))],
        out_specs=pl.BlockSpec((tile, D), lambda i: (i, 0)),
        compiler_params=_cparams(("parallel",)),
        name="rmsnorm",
    )(x, g.reshape(1, D))


def _pool_diff_kernel(meta_ref, xp_ref, x_ref, xn_ref, g_ref, o_ref, hs_ref, *, tile, group):
    i = pl.program_id(0)
    pos0 = meta_ref[i, 0]
    seqlen = meta_ref[i, 1]
    g = g_ref[...]

    def norm(x):
        ms = jnp.mean(x * x, axis=-1, keepdims=True)
        return x * lax.rsqrt(ms + RMS_EPS) * g

    H = POOL_HALO
    first = pos0 == 0
    last = pos0 + tile == seqlen
    hs_ref[H:H + tile, :] = norm(x_ref[...])
    hs_ref[0:H, :] = jnp.where(first, 0.0, norm(xp_ref[...]))
    hs_ref[H + tile:2 * H + tile, :] = jnp.where(last, 0.0, norm(xn_ref[...]))
    pos = pos0 + lax.broadcasted_iota(jnp.int32, (tile, group), 0)
    for gi, w in enumerate(POOL_WINDOWS):
        c0 = gi * group
        lo_off = H - w // 2
        acc = hs_ref[pl.ds(lo_off, tile), c0:c0 + group]
        for k in range(1, w):
            acc = acc + hs_ref[pl.ds(lo_off + k, tile), c0:c0 + group]
        lo = jnp.maximum(pos - w // 2, 0)
        hi = jnp.minimum(pos + (w - 1 - w // 2), seqlen - 1)
        cnt = (hi - lo + 1).astype(F32)
        o_ref[:, c0:c0 + group] = (acc / cnt - hs_ref[H:H + tile, c0:c0 + group]).astype(o_ref.dtype)


def _pool_diff(x, g, seqs, *, tile=256):
    T, D = x.shape
    H = POOL_HALO
    meta = []
    for start, length in seqs:
        for p in range(0, length, tile):
            meta.append((p, length))
    meta = jnp.asarray(np.array(meta, dtype=np.int32))
    nt = T // tile
    per = tile // H
    nh = T // H
    return pl.pallas_call(
        functools.partial(_pool_diff_kernel, tile=tile, group=D // len(POOL_WINDOWS)),
        out_shape=jax.ShapeDtypeStruct((T, D), BF16),
        grid_spec=pltpu.PrefetchScalarGridSpec(
            num_scalar_prefetch=1,
            grid=(nt,),
            in_specs=[pl.BlockSpec((H, D), lambda i, m: (jnp.maximum(i * per - 1, 0), 0)),
                      pl.BlockSpec((tile, D), lambda i, m: (i, 0)),
                      pl.BlockSpec((H, D), lambda i, m: (jnp.minimum((i + 1) * per, nh - 1), 0)),
                      pl.BlockSpec((1, D), lambda i, m: (0, 0))],
            out_specs=pl.BlockSpec((tile, D), lambda i, m: (i, 0)),
            scratch_shapes=[pltpu.VMEM((tile + 2 * H, D), F32)]),
        compiler_params=_cparams(("parallel",)),
        name="pool_diff",
    )(meta, x, x, x, g.reshape(1, D))


def _matmul_kernel(a_ref, w_ref, b_ref, s_ref, *rest, has_res):
    if has_res:
        r_ref, o_ref = rest
    else:
        (o_ref,) = rest
    acc = jnp.dot(a_ref[...], w_ref[...], preferred_element_type=F32)
    y = (acc + b_ref[...]) * s_ref[...]
    if has_res:
        y = r_ref[...] + y
    o_ref[...] = y.astype(o_ref.dtype)


def _matmul(a, w, bias, scale, res, *, tm=512, tn=1024, out_dtype=F32):
    T, K = a.shape
    N = w.shape[1]
    tn = min(tn, N)
    has_res = res is not None
    in_specs = [pl.BlockSpec((tm, K), lambda i, j: (i, 0)),
                pl.BlockSpec((K, tn), lambda i, j: (0, j)),
                pl.BlockSpec((1, tn), lambda i, j: (0, j)),
                pl.BlockSpec((1, tn), lambda i, j: (0, j))]
    args = [a, w, bias.reshape(1, N).astype(F32), scale.reshape(1, N).astype(F32)]
    if has_res:
        in_specs.append(pl.BlockSpec((tm, tn), lambda i, j: (i, j)))
        args.append(res)
    return pl.pallas_call(
        functools.partial(_matmul_kernel, has_res=has_res),
        out_shape=jax.ShapeDtypeStruct((T, N), out_dtype),
        grid=(T // tm, N // tn),
        in_specs=in_specs,
        out_specs=pl.BlockSpec((tm, tn), lambda i, j: (i, j)),
        compiler_params=_cparams(("parallel", "parallel")),
        name="matmul_epilogue",
    )(*args)


def _route_kernel(xn_ref, wq_ref, keys_ref, r2_ref, e2_ref, cc_ref, e1n_ref,
                  q_scr, sv1_scr, sv2_scr, cand_scr, *, n_heads, n_keys, half):
    K = PEER_TOPK
    tt = xn_ref.shape[0]
    q_scr[...] = lax.dot_general(wq_ref[...], xn_ref[...], (((1,), (1,)), ((), ())),
                                 preferred_element_type=F32)

    def top_k_rows(s, sv_scr, want_rank):
        rank = jnp.full(s.shape, float(K), F32)
        for it in range(K):
            m = jnp.max(s, axis=0, keepdims=True)
            eq = s == m
            rank = jnp.where(eq, float(it), rank)
            s = jnp.where(eq, NEG_INF, s)
            sv_scr[it:it + 1, :] = m
        return rank

    n_single = K - K // 2
    cand_rows = cand_scr.shape[0]

    def head(h, carry):
        base = h * 2 * half
        q1 = q_scr[pl.ds(base, half), :].astype(BF16)
        q2 = q_scr[pl.ds(base + half, half), :].astype(BF16)
        s1 = jnp.dot(keys_ref[h, 0], q1, preferred_element_type=F32)
        s2 = jnp.dot(keys_ref[h, 1], q2, preferred_element_type=F32)
        rank1 = top_k_rows(s1, sv1_scr, True)
        rank2 = top_k_rows(s2, sv2_scr, True)

        row = 0
        cand_scr[0:K, :] = sv1_scr[0:1, :] + sv2_scr[0:K, :]
        row = K
        for r in range(1, K // 2):
            cmax = K // (r + 1)
            blk = sv1_scr[r:r + 1, :] + sv2_scr[0:8, :]
            ridx = lax.broadcasted_iota(jnp.int32, (8, tt), 0)
            cand_scr[row:row + 8, :] = jnp.where(ridx < cmax, blk, NEG_INF)
            row += 8
        cand_scr[row:row + n_single, :] = sv1_scr[K // 2:K, :] + sv2_scr[0:1, :]
        row += n_single
        assert row == cand_rows
        cand = cand_scr[...]
        c = cand
        tau = None
        for it in range(K):
            tau = jnp.max(c, axis=0, keepdims=True)
            c = jnp.where(c == tau, NEG_INF, c)
        m0 = sv1_scr[0:1, :] + sv2_scr[0:1, :]
        z = jnp.sum(jnp.where(cand >= tau, jnp.exp(cand - m0), 0.0), axis=0, keepdims=True)

        cc = jnp.zeros(s1.shape, F32)
        for cidx in range(K):
            cc = cc + jnp.where(s1 + sv2_scr[cidx:cidx + 1, :] >= tau, 1.0, 0.0)
        cc = jnp.where(rank1 < float(K), cc, 0.0)
        e1n = jnp.exp(s1 - sv1_scr[0:1, :]) / z
        e2 = jnp.exp(s2 - sv2_scr[0:1, :])
        r2_ref[h] = rank2.astype(r2_ref.dtype)
        e2_ref[h] = e2.astype(e2_ref.dtype)
        cc_ref[h] = cc
        e1n_ref[h] = e1n
        return carry

    lax.fori_loop(0, n_heads, head, 0)


def _peer_route(xn, wq_t, keys, *, tt=256):
    T, D = xn.shape
    n_heads, _, n_keys, half = keys.shape
    K = PEER_TOPK
    cand_rows = K + 8 * (K // 2 - 1) + (K - K // 2)
    shp = (n_heads, n_keys, T)
    blk = pl.BlockSpec((n_heads, n_keys, tt), lambda i: (0, 0, i))
    return pl.pallas_call(
        functools.partial(_route_kernel, n_heads=n_heads, n_keys=n_keys, half=half),
        out_shape=(jax.ShapeDtypeStruct(shp, BF16), jax.ShapeDtypeStruct(shp, BF16),
                   jax.ShapeDtypeStruct(shp, F32), jax.ShapeDtypeStruct(shp, F32)),
        grid=(T // tt,),
        in_specs=[pl.BlockSpec((tt, D), lambda i: (i, 0)),
                  pl.BlockSpec(wq_t.shape, lambda i: (0, 0)),
                  pl.BlockSpec(keys.shape, lambda i: (0, 0, 0, 0))],
        out_specs=(blk, blk, blk, blk),
        scratch_shapes=[pltpu.VMEM((wq_t.shape[0], tt), F32),
                        pltpu.VMEM((K, tt), F32), pltpu.VMEM((K, tt), F32),
                        pltpu.VMEM((cand_rows, tt), F32)],
        compiler_params=_cparams(("parallel",)),
        name="peer_route",
    )(xn, wq_t, keys)


def _peer_main_kernel(xn_ref, u_ref, v_ref, r2_ref, e2_ref, cc_ref, e1n_ref, res_ref, o_ref, a_scr,
                      *, n_heads, n_keys, rows_per_step):
    e = pl.program_id(1)
    tt = xn_ref.shape[0]
    CH = DFT_CHUNK

    @pl.when(e == 0)
    def _():
        o_ref[...] = res_ref[...]

    s_t = lax.dot_general(u_ref[...], xn_ref[...], (((1,), (1,)), ((), ())),
                          preferred_element_type=F32)
    for ii in range(rows_per_step):
        i_glob = e * rows_per_step + ii
        ccb = [jnp.broadcast_to(cc_ref[h, pl.ds(i_glob, 1), :], (CH, tt)).astype(BF16)
               for h in range(n_heads)]
        e1b = [jnp.broadcast_to(e1n_ref[h, pl.ds(i_glob, 1), :], (CH, tt)).astype(BF16)
               for h in range(n_heads)]
        for jb in range(n_keys // CH):
            j0 = jb * CH
            g = jnp.zeros((CH, tt), BF16)
            for h in range(n_heads):
                r2 = r2_ref[h, j0:j0 + CH, :]
                e2 = e2_ref[h, j0:j0 + CH, :]
                g = g + jnp.where(r2 < ccb[h], e2 * e1b[h], jnp.zeros_like(e2))
            s = s_t[ii * n_keys + j0: ii * n_keys + j0 + CH, :]
            gel = 0.5 * s * (1.0 + lax.erf(s * INV_SQRT2))
            a_scr[ii * n_keys + j0: ii * n_keys + j0 + CH, :] = gel.astype(BF16) * g
    o_ref[...] += lax.dot_general(a_scr[...], v_ref[...], (((0,), (0,)), ((), ())),
                                  preferred_element_type=F32)


def _peer_main(xn, u, v, tables, res, *, tt=512, eb=512):
    T, D = xn.shape
    E = u.shape[0]
    r2, e2, cc, e1n = tables
    n_heads, n_keys, _ = r2.shape
    tab = pl.BlockSpec((n_heads, n_keys, tt), lambda i, e: (0, 0, i))
    return pl.pallas_call(
        functools.partial(_peer_main_kernel, n_heads=n_heads, n_keys=n_keys, rows_per_step=eb // n_keys),
        out_shape=jax.ShapeDtypeStruct((T, D), F32),
        grid=(T // tt, E // eb),
        in_specs=[pl.BlockSpec((tt, D), lambda i, e: (i, 0)),
                  pl.BlockSpec((eb, D), lambda i, e: (e, 0)),
                  pl.BlockSpec((eb, D), lambda i, e: (e, 0)),
                  tab, tab, tab, tab,
                  pl.BlockSpec((tt, D), lambda i, e: (i, 0))],
        out_specs=pl.BlockSpec((tt, D), lambda i, e: (i, 0)),
        scratch_shapes=[pltpu.VMEM((eb, tt), BF16)],
        compiler_params=_cparams(("parallel", "arbitrary")),
        name="peer_main",
    )(xn, u, v, r2, e2, cc, e1n, res)


def _peer_layer(x, gain, w_q, sub_keys, u, v):
    D = x.shape[1]
    xn = _rmsnorm(x, gain, BF16)
    tables = _peer_route(xn, w_q.T.astype(BF16), sub_keys.astype(BF16))
    return _peer_main(xn, u.astype(BF16), v.astype(BF16), tables, x)


def _filter_kernel(w1t_ref, w1c_ref, w1s_ref, b1_ref, fr1_ref, w2_ref, b2_ref, fr2_ref, w3_ref, two_ref, abs_ref,
                   *, L, D):
    i = pl.program_id(0)
    rt = two_ref.shape[0]
    hp = lax.Precision.HIGHEST
    rho = i * rt + lax.broadcasted_iota(jnp.int32, (rt, 1), 0)
    tau = jnp.where(rho < L, rho, 2 * L - rho).astype(F32)
    t = tau / float(L - 1)
    ang_base = (2.0 * math.pi) * tau / float(L)
    bidx = lax.broadcasted_iota(jnp.int32, (1, FILTER_BANDS), 1).astype(F32)
    bands = 1e-4 + bidx * ((FILTER_BANDS - 1 - 1e-4) / (FILTER_BANDS - 1))
    ang = ang_base * bands
    pre = (t * w1t_ref[...]
           + jnp.dot(jnp.cos(ang), w1c_ref[...], precision=hp, preferred_element_type=F32)
           - jnp.dot(jnp.sin(ang), w1s_ref[...], precision=hp, preferred_element_type=F32)
           + b1_ref[...])
    h = jnp.sin(fr1_ref[...] * pre)
    h = jnp.sin(fr2_ref[...] * (jnp.dot(h, w2_ref[...], precision=hp, preferred_element_type=F32) + b2_ref[...]))
    k = jnp.dot(h, w3_ref[...], precision=hp, preferred_element_type=F32)
    max_decay = math.log(DECAY_TARGET) / FAST_DECAY_PCT
    min_decay = math.log(DECAY_TARGET) / SLOW_DECAY_PCT
    didx = lax.broadcasted_iota(jnp.int32, (1, D), 1).astype(F32)
    deltas = jnp.abs(min_decay + didx * ((max_decay - min_decay) / (D - 1)))
    decay = jnp.exp(-t * deltas)
    fwd = jnp.where(rho < L, k[:, :D], 0.0)
    bwd = jnp.where((rho > L) | (rho == 0), k[:, D:], 0.0)
    two = (fwd + bwd) * decay
    two_ref[...] = two

    @pl.when(i == 0)
    def _():
        abs_ref[...] = jnp.zeros_like(abs_ref)

    abs_ref[...] += jnp.sum(jnp.abs(two), axis=0, keepdims=True)


def _hyena_filter(L, D, w1, b1, fr1, w2, b2, fr2, w3, *, rt=256):
    full = lambda a: pl.BlockSpec(a.shape, lambda i: (0,) * a.ndim)
    nb = FILTER_BANDS
    args = [w1[0:1], w1[1:1 + nb], w1[1 + nb:1 + 2 * nb], b1.reshape(1, -1), fr1.reshape(1, -1),
            w2, b2.reshape(1, -1), fr2.reshape(1, -1), w3]
    return pl.pallas_call(
        functools.partial(_filter_kernel, L=L, D=D),
        out_shape=(jax.ShapeDtypeStruct((2 * L, D), F32), jax.ShapeDtypeStruct((1, D), F32)),
        grid=(2 * L // rt,),
        in_specs=[full(a) for a in args],
        out_specs=(pl.BlockSpec((rt, D), lambda i: (i, 0)), pl.BlockSpec((1, D), lambda i: (0, 0))),
        compiler_params=_cparams(("arbitrary",)),
        name="hyena_filter",
    )(*args)


def _dft_constants(N, n_in_blocks):
    R, C = DFT_R, DFT_CHUNK
    A = N // R
    ka = np.arange(A)
    r0 = np.arange(C)
    a_in = np.arange(n_in_blocks)
    coef = np.exp(-2j * np.pi * (ka[:, None, None] * a_in[None, None, :] / A + r0[None, :, None] * ka[:, None, None] / N))
    m1 = np.zeros((2, A, C, n_in_blocks, C))
    for q in range(C):
        m1[0, :, q, :, q] = coef[:, q, :].real
        m1[1, :, q, :, q] = coef[:, q, :].imag
    m1 = m1.reshape(2 * A * C, n_in_blocks * C)
    icoef = np.conj(coef) / N
    m3 = np.zeros((n_in_blocks, C, 2, A, C))
    for q in range(C):
        m3[:, q, 0, :, q] = icoef[:, q, :].real.T
        m3[:, q, 1, :, q] = -icoef[:, q, :].imag.T
    m3 = m3.reshape(n_in_blocks * C, 2 * A * C)
    kr = np.arange(R)
    F = np.exp(-2j * np.pi * np.outer(kr, kr) / R)
    f2 = np.block([[F.real, -F.imag], [F.imag, F.real]])
    Fi = np.conj(F)
    i1 = np.block([[Fi.real, -Fi.imag], [Fi.imag, Fi.real]])
    r1 = np.arange(R // C)
    tw = np.exp(-2j * np.pi * C * np.outer(ka, r1) / N)
    return m1, f2, i1, m3, tw


def _twiddle(re, im, w):
    wr, wi = float(w.real), float(w.imag)
    if abs(wr - 1.0) < 1e-15 and abs(wi) < 1e-15:
        return re, im
    return re * wr - im * wi, re * wi + im * wr


def _forward_dft(xb_ref, n_in_blocks, m1_ref, f2_ref, tw, y_scr, emit):
    R, C = DFT_R, DFT_CHUNK
    A = tw.shape[0]
    m1 = m1_ref[...]
    for r1 in range(R // C):
        xin = jnp.concatenate([xb_ref[a * R + r1 * C: a * R + (r1 + 1) * C, :] for a in range(n_in_blocks)], axis=0)
        y = jnp.dot(m1, xin, preferred_element_type=F32)
        for k in range(A):
            re = y[k * C:(k + 1) * C, :]
            im = y[(A + k) * C:(A + k + 1) * C, :]
            re, im = _twiddle(re, im, tw[k, r1])
            y_scr[k, r1 * C:(r1 + 1) * C, :] = re.astype(BF16)
            y_scr[k, R + r1 * C:R + (r1 + 1) * C, :] = im.astype(BF16)
    f2 = f2_ref[...]
    for k in range(A):
        emit(k, jnp.dot(f2, y_scr[k], preferred_element_type=F32))


def _spectrum_kernel(two_ref, abs_ref, m1_ref, f2_ref, kf_ref, xb_scr, y_scr, *, tw):
    A = tw.shape[0]
    xb_scr[...] = two_ref[...].astype(BF16)
    inv = 1.0 / abs_ref[...]

    def emit(k, X):
        kf_ref[k] = X * inv

    _forward_dft(xb_scr, A, m1_ref, f2_ref, tw, y_scr, emit)


def _filter_spectrum(two, abssum, consts, *, dt=128):
    N, D = two.shape
    R = DFT_R
    A = N // R
    m1, f2, _, _, tw = consts
    m1 = jnp.asarray(m1, BF16)
    f2 = jnp.asarray(f2, BF16)
    return pl.pallas_call(
        functools.partial(_spectrum_kernel, tw=tw),
        out_shape=jax.ShapeDtypeStruct((A, 2 * R, D), F32),
        grid=(D // dt,),
        in_specs=[pl.BlockSpec((N, dt), lambda j: (0, j)),
                  pl.BlockSpec((1, dt), lambda j: (0, j)),
                  pl.BlockSpec(m1.shape, lambda j: (0, 0)),
                  pl.BlockSpec(f2.shape, lambda j: (0, 0))],
        out_specs=pl.BlockSpec((A, 2 * R, dt), lambda j: (0, 0, j)),
        scratch_shapes=[pltpu.VMEM((N, dt), BF16), pltpu.VMEM((A, 2 * R, dt), BF16)],
        compiler_params=_cparams(("parallel",)),
        name="hyena_filter_spectrum",
    )(two, abssum, m1, f2)


def _conv_kernel(u0_ref, u1_ref, u2_ref, s0_ref, s1_ref, s2_ref, skip_ref, kf_ref,
                 m1_ref, f2_ref, i1_ref, m3_ref, o_ref,
                 z_scr, zb_scr, x0_scr, y_scr, q_scr, *, L, tw):
    R, C = DFT_R, DFT_CHUNK
    A = tw.shape[0]
    nb = L // R
    dt = o_ref.shape[1]
    ridx = lax.broadcasted_iota(jnp.int32, (L, dt), 0)

    def short_conv(u_ref, s_ref):
        u = u_ref[...]
        prev = jnp.where(ridx == 0, 0.0, pltpu.roll(u, 1, axis=0))
        nxt = jnp.where(ridx == L - 1, 0.0, pltpu.roll(u, L - 1, axis=0))
        return prev * s_ref[0:1, :] + u * s_ref[1:2, :] + nxt * s_ref[2:3, :]

    x0_scr[...] = short_conv(u0_ref, s0_ref)
    z = short_conv(u2_ref, s2_ref) * short_conv(u1_ref, s1_ref)
    z_scr[...] = z
    zb_scr[...] = z.astype(BF16)

    i1 = i1_ref[...]

    def emit(k, X):
        xr, xi = X[:R, :], X[R:, :]
        kr, ki = kf_ref[k, 0:R, :], kf_ref[k, R:2 * R, :]
        p = jnp.concatenate([xr * kr - xi * ki, xr * ki + xi * kr], axis=0).astype(BF16)
        q = jnp.dot(i1, p, preferred_element_type=F32)
        for r1 in range(R // C):
            re = q[r1 * C:(r1 + 1) * C, :]
            im = q[R + r1 * C:R + (r1 + 1) * C, :]
            re, im = _twiddle(re, im, np.conj(tw[k, r1]))
            q_scr[r1, k * C:(k + 1) * C, :] = re.astype(BF16)
            q_scr[r1, (A + k) * C:(A + k + 1) * C, :] = im.astype(BF16)

    _forward_dft(zb_scr, nb, m1_ref, f2_ref, tw, y_scr, emit)

    m3 = m3_ref[...]
    skip = skip_ref[...]
    for r1 in range(R // C):
        o = jnp.dot(m3, q_scr[r1], preferred_element_type=F32)
        for a in range(nb):
            rows = slice(a * R + r1 * C, a * R + (r1 + 1) * C)
            y = (o[a * C:(a + 1) * C, :] + z_scr[rows, :] * skip) * x0_scr[rows, :]
            o_ref[rows, :] = y.astype(o_ref.dtype)


def _hyena_conv(u, short, skip, kf, consts, *, row_start, batch, L, dt=128):
    D = u.shape[1] // 3
    R, C = DFT_R, DFT_CHUNK
    A = 2 * L // R
    m1, f2, i1, m3, tw = consts
    m1, f2, i1, m3 = (jnp.asarray(m, BF16) for m in (m1, f2, i1, m3))
    nd = D // dt
    b0 = row_start // L
    const = lambda m: pl.BlockSpec(m.shape, lambda j, b: (0, 0))
    return pl.pallas_call(
        functools.partial(_conv_kernel, L=L, tw=tw),
        out_shape=jax.ShapeDtypeStruct((batch * L, D), BF16),
        grid=(nd, batch),
        in_specs=[pl.BlockSpec((L, dt), lambda j, b: (b0 + b, j)),
                  pl.BlockSpec((L, dt), lambda j, b: (b0 + b, nd + j)),
                  pl.BlockSpec((L, dt), lambda j, b: (b0 + b, 2 * nd + j)),
                  pl.BlockSpec((3, dt), lambda j, b: (0, j)),
                  pl.BlockSpec((3, dt), lambda j, b: (0, nd + j)),
                  pl.BlockSpec((3, dt), lambda j, b: (0, 2 * nd + j)),
                  pl.BlockSpec((1, dt), lambda j, b: (0, j)),
                  pl.BlockSpec((A, 2 * R, dt), lambda j, b: (0, 0, j)),
                  const(m1), const(f2), const(i1), const(m3)],
        out_specs=pl.BlockSpec((L, dt), lambda j, b: (b, j)),
        scratch_shapes=[pltpu.VMEM((L, dt), F32), pltpu.VMEM((L, dt), BF16), pltpu.VMEM((L, dt), F32),
                        pltpu.VMEM((A, 2 * R, dt), BF16), pltpu.VMEM((R // C, 2 * A * C, dt), BF16)],
        compiler_params=_cparams(("parallel", "arbitrary")),
        name="hyena_conv",
    )(u, u, u, short, short, short, skip.reshape(1, D), kf, m1, f2, i1, m3)


def _hyena_layer(x, gain, seqs, w_in, b_in, w_short, f_w1, f_b1, f_fr1, f_w2, f_b2, f_fr2, f_w3, skip, w_out, b_out):
    T, D = x.shape
    h = _rmsnorm(x, gain, BF16)
    ones3 = jnp.ones((3 * D,), F32)
    u = _matmul(h, w_in.astype(BF16), b_in, ones3, None)
    groups = {}
    for start, length in seqs:
        groups.setdefault(length, []).append(start)
    ys = []
    for length, starts in groups.items():
        assert starts == list(range(starts[0], starts[0] + length * len(starts), length))
        two, abssum = _hyena_filter(length, D, f_w1, f_b1, f_fr1, f_w2, f_b2, f_fr2, f_w3)
        kf = _filter_spectrum(two, abssum, _dft_constants(2 * length, 2 * length // DFT_R))
        ys.append(_hyena_conv(u, w_short, skip, kf, _dft_constants(2 * length, length // DFT_R),
                              row_start=starts[0], batch=len(starts), L=length))
    y = jnp.concatenate(ys, axis=0) if len(ys) > 1 else ys[0]
    return _matmul(y, w_out.astype(BF16), b_out, jnp.ones((D,), F32), x)


def _pool_layer(x, gain, seqs, w, scale):
    D = x.shape[1]
    diff = _pool_diff(x, gain, seqs)
    return _matmul(diff, w.reshape(D, D).astype(BF16), jnp.zeros((D,), F32), scale, x)


def kernel(x_prompt, x_sample, mix_gain, ffn_gain, final_gain, pool_w, pool_scale, hy_w_in, hy_b_in, hy_short, hy_f_w1, hy_f_b1, hy_f_freq1, hy_f_w2, hy_f_b2, hy_f_freq2, hy_f_w3, hy_skip, hy_w_out, hy_b_out, peer_w_q, peer_sub_keys, peer_u, peer_v):
    D = x_prompt.shape[-1]
    depth = mix_gain.shape[0]
    seqs = []
    row = 0
    for xs in (x_prompt, x_sample):
        for _ in range(xs.shape[0]):
            seqs.append((row, xs.shape[1]))
            row += xs.shape[1]
    seqs = tuple(seqs)
    x = jnp.concatenate([x_prompt.reshape(-1, D), x_sample.reshape(-1, D)], axis=0)
    for i in range(depth):
        j = i // 2
        if i % 2 == 0:
            x = _pool_layer(x, mix_gain[i], seqs, pool_w[j], pool_scale[j])
        else:
            x = _hyena_layer(x, mix_gain[i], seqs, hy_w_in[j], hy_b_in[j], hy_short[j], hy_f_w1[j], hy_f_b1[j],
                             hy_f_freq1[j], hy_f_w2[j], hy_f_b2[j], hy_f_freq2[j], hy_f_w3[j], hy_skip[j],
                             hy_w_out[j], hy_b_out[j])
        x = _peer_layer(x, ffn_gain[i], peer_w_q[i], peer_sub_keys[i], peer_u[i], peer_v[i])
    n_p = x_prompt.shape[0] * x_prompt.shape[1]
    n_s = x_sample.shape[0] * x_sample.shape[1]
    y_p = _rmsnorm(x, final_gain, F32, row_start=0, n_rows=n_p)
    y_s = _rmsnorm(x, final_gain, F32, row_start=n_p, n_rows=n_s)
    return (y_p.reshape(x_prompt.shape), y_s.reshape(x_sample.shape))
```

```python
import functools
import math

import numpy as np
import jax
import jax.numpy as jnp
from jax import lax
from jax.experimental import pallas as pl
from jax.experimental.pallas import tpu as pltpu

F32 = jnp.float32
BF16 = jnp.bfloat16

RMS_EPS = 1e-6
POOL_WINDOWS = (2, 4, 8, 16)
POOL_HALO = 8
PEER_TOPK = 16
FILTER_BANDS = 16
DECAY_TARGET = 1e-2
FAST_DECAY_PCT = 0.3
SLOW_DECAY_PCT = 1.5
INV_SQRT2 = 0.7071067811865476
NEG_INF = float("-inf")

DFT_R = 256
DFT_CHUNK = 16
VMEM_LIMIT = 56 * 1024 * 1024


def _cparams(sem):
    return pltpu.CompilerParams(dimension_semantics=sem, vmem_limit_bytes=VMEM_LIMIT)


def _rms_kernel(x_ref, g_ref, o_ref):
    x = x_ref[...]
    ms = jnp.mean(x * x, axis=-1, keepdims=True)
    o_ref[...] = (x * lax.rsqrt(ms + RMS_EPS) * g_ref[...]).astype(o_ref.dtype)


def _rms_t_kernel(x_ref, g_ref, o_ref):
    x = x_ref[...]
    ms = jnp.mean(x * x, axis=-1, keepdims=True)
    o_ref[...] = (x * lax.rsqrt(ms + RMS_EPS) * g_ref[...]).T.astype(o_ref.dtype)


def _rmsnorm_t(x, g, out_dtype, *, tile=256):
    T, D = x.shape
    return pl.pallas_call(
        _rms_t_kernel,
        out_shape=jax.ShapeDtypeStruct((D, T), out_dtype),
        grid=(T // tile,),
        in_specs=[pl.BlockSpec((tile, D), lambda i: (i, 0)),
                  pl.BlockSpec((1, D), lambda i: (0, 0))],
        out_specs=pl.BlockSpec((D, tile), lambda i: (0, i)),
        compiler_params=_cparams(("parallel",)),
        name="rmsnorm_t",
    )(x, g.reshape(1, D))


def _rmsnorm(x, g, out_dtype, *, row_start=0, n_rows=None, tile=256):
    T, D = x.shape
    n_rows = T if n_rows is None else n_rows
    off = row_start // tile
    return pl.pallas_call(
        _rms_kernel,
        out_shape=jax.ShapeDtypeStruct((n_rows, D), out_dtype),
        grid=(n_rows // tile,),
        in_specs=[pl.BlockSpec((tile, D), lambda i: (i + off, 0)),
                  pl.BlockSpec((1, D), lambda i: (0, 0))],
        out_specs=pl.BlockSpec((tile, D), lambda i: (i, 0)),
        compiler_params=_cparams(("parallel",)),
        name="rmsnorm",
    )(x, g.reshape(1, D))


def _pool_diff_kernel(meta_ref, xp_ref, x_ref, xn_ref, g_ref, o_ref, hs_ref, *, tile, group):
    i = pl.program_id(0)
    pos0 = meta_ref[i, 0]
    seqlen = meta_ref[i, 1]
    g = g_ref[...]

    def norm(x):
        ms = jnp.mean(x * x, axis=-1, keepdims=True)
        return x * lax.rsqrt(ms + RMS_EPS) * g

    H = POOL_HALO
    first = pos0 == 0
    last = pos0 + tile == seqlen
    hs_ref[H:H + tile, :] = norm(x_ref[...])
    hs_ref[0:H, :] = jnp.where(first, 0.0, norm(xp_ref[...]))
    hs_ref[H + tile:2 * H + tile, :] = jnp.where(last, 0.0, norm(xn_ref[...]))
    pos = pos0 + lax.broadcasted_iota(jnp.int32, (tile, group), 0)
    for gi, w in enumerate(POOL_WINDOWS):
        c0 = gi * group
        lo_off = H - w // 2
        acc = hs_ref[pl.ds(lo_off, tile), c0:c0 + group]
        for k in range(1, w):
            acc = acc + hs_ref[pl.ds(lo_off + k, tile), c0:c0 + group]
        lo = jnp.maximum(pos - w // 2, 0)
        hi = jnp.minimum(pos + (w - 1 - w // 2), seqlen - 1)
        cnt = (hi - lo + 1).astype(F32)
        o_ref[:, c0:c0 + group] = (acc / cnt - hs_ref[H:H + tile, c0:c0 + group]).astype(o_ref.dtype)


def _pool_diff(x, g, seqs, *, tile=256):
    T, D = x.shape
    H = POOL_HALO
    meta = []
    for start, length in seqs:
        for p in range(0, length, tile):
            meta.append((p, length))
    meta = jnp.asarray(np.array(meta, dtype=np.int32))
    nt = T // tile
    per = tile // H
    nh = T // H
    return pl.pallas_call(
        functools.partial(_pool_diff_kernel, tile=tile, group=D // len(POOL_WINDOWS)),
        out_shape=jax.ShapeDtypeStruct((T, D), BF16),
        grid_spec=pltpu.PrefetchScalarGridSpec(
            num_scalar_prefetch=1,
            grid=(nt,),
            in_specs=[pl.BlockSpec((H, D), lambda i, m: (jnp.maximum(i * per - 1, 0), 0)),
                      pl.BlockSpec((tile, D), lambda i, m: (i, 0)),
                      pl.BlockSpec((H, D), lambda i, m: (jnp.minimum((i + 1) * per, nh - 1), 0)),
                      pl.BlockSpec((1, D), lambda i, m: (0, 0))],
            out_specs=pl.BlockSpec((tile, D), lambda i, m: (i, 0)),
            scratch_shapes=[pltpu.VMEM((tile + 2 * H, D), F32)]),
        compiler_params=_cparams(("parallel",)),
        name="pool_diff",
    )(meta, x, x, x, g.reshape(1, D))


def _matmul_kernel(a_ref, w_ref, b_ref, s_ref, *rest, has_res):
    if has_res:
        r_ref, o_ref = rest
    else:
        (o_ref,) = rest
    acc = jnp.dot(a_ref[...], w_ref[...], preferred_element_type=F32)
    y = (acc + b_ref[...]) * s_ref[...]
    if has_res:
        y = r_ref[...] + y
    o_ref[...] = y.astype(o_ref.dtype)


def _matmul(a, w, bias, scale, res, *, tm=512, tn=2048, out_dtype=F32):
    T, K = a.shape
    N = w.shape[1]
    tn = min(tn, N)
    has_res = res is not None
    in_specs = [pl.BlockSpec((tm, K), lambda j, i: (i, 0)),
                pl.BlockSpec((K, tn), lambda j, i: (0, j)),
                pl.BlockSpec((1, tn), lambda j, i: (0, j)),
                pl.BlockSpec((1, tn), lambda j, i: (0, j))]
    args = [a, w, bias.reshape(1, N).astype(F32), scale.reshape(1, N).astype(F32)]
    if has_res:
        in_specs.append(pl.BlockSpec((tm, tn), lambda j, i: (i, j)))
        args.append(res)
    return pl.pallas_call(
        functools.partial(_matmul_kernel, has_res=has_res),
        out_shape=jax.ShapeDtypeStruct((T, N), out_dtype),
        grid=(N // tn, T // tm),
        in_specs=in_specs,
        out_specs=pl.BlockSpec((tm, tn), lambda j, i: (i, j)),
        compiler_params=_cparams(("parallel", "parallel")),
        name="matmul_epilogue",
    )(*args)


ROUTE_HEADS_PER_ITER = 2


def _route_kernel(xnt_ref, wq_ref, keys_ref, r2_ref, e2_ref, cc_ref, e1n_ref,
                  q_scr, sv_scr, cand_scr, *, n_heads, n_keys, half):
    K = PEER_TOPK
    tt = xnt_ref.shape[1]
    q_scr[...] = jnp.dot(wq_ref[...], xnt_ref[...], preferred_element_type=F32)

    def top_k_rows(s, sv_ref, want_rank):
        rank = jnp.full(s.shape, float(K), F32) if want_rank else None
        for it in range(K):
            m = jnp.max(s, axis=0, keepdims=True)
            eq = s == m
            if want_rank:
                rank = jnp.where(eq, float(it), rank)
            s = jnp.where(eq, NEG_INF, s)
            sv_ref[it:it + 1, :] = m
        return rank

    n_single = K - K // 2
    cand_rows = cand_scr.shape[1]

    def one_head(h, slot):
        sv1 = sv_scr.at[2 * slot]
        sv2 = sv_scr.at[2 * slot + 1]
        cand_ref = cand_scr.at[slot]
        base = h * 2 * half
        q1 = q_scr[pl.ds(base, half), :].astype(BF16)
        q2 = q_scr[pl.ds(base + half, half), :].astype(BF16)
        s1 = jnp.dot(keys_ref[h, 0], q1, preferred_element_type=F32)
        s2 = jnp.dot(keys_ref[h, 1], q2, preferred_element_type=F32)
        top_k_rows(s1, sv1, False)
        rank2 = top_k_rows(s2, sv2, True)

        cand_ref[0:K, :] = sv1[0:1, :] + sv2[0:K, :]
        row = K
        for r in range(1, K // 2):
            cmax = K // (r + 1)
            blk = sv1[r:r + 1, :] + sv2[0:8, :]
            ridx = lax.broadcasted_iota(jnp.int32, (8, tt), 0)
            cand_ref[row:row + 8, :] = jnp.where(ridx < cmax, blk, NEG_INF)
            row += 8
        cand_ref[row:row + n_single, :] = sv1[K // 2:K, :] + sv2[0:1, :]
        row += n_single
        assert row == cand_rows
        cand = cand_ref[...]
        c = cand
        tau = None
        for it in range(K):
            tau = jnp.max(c, axis=0, keepdims=True)
            c = jnp.where(c == tau, NEG_INF, c)
        m0 = sv1[0:1, :] + sv2[0:1, :]
        z = jnp.sum(jnp.where(cand >= tau, jnp.exp(cand - m0), 0.0), axis=0, keepdims=True)

        cc = jnp.zeros(s1.shape, F32)
        for cidx in range(K):
            cc = cc + jnp.where(s1 + sv2[cidx:cidx + 1, :] >= tau, 1.0, 0.0)
        cc = jnp.where(s1 >= sv1[K - 1:K, :], cc, 0.0)
        e1n = jnp.exp(s1 - sv1[0:1, :]) / z
        e2 = jnp.exp(s2 - sv2[0:1, :])
        r2_ref[h] = rank2.astype(r2_ref.dtype)
        e2_ref[h] = e2.astype(e2_ref.dtype)
        cc_ref[h] = cc
        e1n_ref[h] = e1n

    def heads(p, carry):
        for slot in range(ROUTE_HEADS_PER_ITER):
            one_head(p * ROUTE_HEADS_PER_ITER + slot, slot)
        return carry

    lax.fori_loop(0, n_heads // ROUTE_HEADS_PER_ITER, heads, 0)


def _peer_route(xnt, wq_t, keys, *, tt=256):
    D, T = xnt.shape
    n_heads, _, n_keys, half = keys.shape
    assert n_heads % ROUTE_HEADS_PER_ITER == 0
    K = PEER_TOPK
    cand_rows = K + 8 * (K // 2 - 1) + (K - K // 2)
    shp = (n_heads, n_keys, T)
    blk = pl.BlockSpec((n_heads, n_keys, tt), lambda i: (0, 0, i))
    return pl.pallas_call(
        functools.partial(_route_kernel, n_heads=n_heads, n_keys=n_keys, half=half),
        out_shape=(jax.ShapeDtypeStruct(shp, BF16), jax.ShapeDtypeStruct(shp, BF16),
                   jax.ShapeDtypeStruct(shp, F32), jax.ShapeDtypeStruct(shp, F32)),
        grid=(T // tt,),
        in_specs=[pl.BlockSpec((D, tt), lambda i: (0, i)),
                  pl.BlockSpec(wq_t.shape, lambda i: (0, 0)),
                  pl.BlockSpec(keys.shape, lambda i: (0, 0, 0, 0))],
        out_specs=(blk, blk, blk, blk),
        scratch_shapes=[pltpu.VMEM((wq_t.shape[0], tt), F32),
                        pltpu.VMEM((2 * ROUTE_HEADS_PER_ITER, K, tt), F32),
                        pltpu.VMEM((ROUTE_HEADS_PER_ITER, cand_rows, tt), F32)],
        compiler_params=_cparams(("parallel",)),
        name="peer_route",
    )(xnt, wq_t, keys)


def _peer_main_kernel(xnt_ref, u_ref, va_ref, vb_ref, r2_ref, e2_ref, cc_ref, e1n_ref, res_ref, o_ref,
                      a0_scr, a1_scr, s_scr, *, n_heads, n_keys, rows_per_block, n_steps):
    e = pl.program_id(1)
    tt = xnt_ref.shape[1]
    CH = DFT_CHUNK
    eb = rows_per_block * n_keys
    dc = o_ref.shape[1] // rows_per_block

    def add_values(a_scr, v_ref, k):
        cols = slice(k * dc, (k + 1) * dc)
        o_ref[:, cols] += lax.dot_general(a_scr[...], v_ref[:, cols], (((0,), (0,)), ((), ())),
                                          preferred_element_type=F32)

    def gated_row(blk, ii, a_scr):
        i_glob = (2 * e + blk) * rows_per_block + ii
        ccb = [jnp.broadcast_to(cc_ref[h, pl.ds(i_glob, 1), :], (CH, tt)).astype(BF16)
               for h in range(n_heads)]
        e1b = [jnp.broadcast_to(e1n_ref[h, pl.ds(i_glob, 1), :], (CH, tt)).astype(BF16)
               for h in range(n_heads)]
        for jb in range(n_keys // CH):
            rows = slice(ii * n_keys + jb * CH, ii * n_keys + (jb + 1) * CH)
            g = jnp.zeros((CH, tt), BF16)
            for h in range(n_heads):
                r2 = r2_ref[h, jb * CH:(jb + 1) * CH, :]
                e2 = e2_ref[h, jb * CH:(jb + 1) * CH, :]
                g = g + jnp.where(r2 < ccb[h], e2 * e1b[h], jnp.zeros_like(e2))
            s = s_scr[blk, rows, :]
            gel = 0.5 * s * (1.0 + lax.erf(s * INV_SQRT2))
            a_scr[rows, :] = gel.astype(BF16) * g

    @pl.when(e == 0)
    def _():
        o_ref[...] = res_ref[...]
        a1_scr[...] = jnp.zeros_like(a1_scr)

    @pl.when(e < n_steps)
    def _():
        s_scr[0] = jnp.dot(u_ref[0:eb, :], xnt_ref[...], preferred_element_type=F32)
        s_scr[1] = jnp.dot(u_ref[eb:2 * eb, :], xnt_ref[...], preferred_element_type=F32)

    @pl.when(e + 1 <= n_steps)
    def _():
        for k in range(rows_per_block):
            add_values(a1_scr, va_ref, k)
            gated_row(0, k, a0_scr)

    @pl.when(e + 2 <= n_steps + 1)
    def _():
        for k in range(rows_per_block):
            add_values(a0_scr, vb_ref, k)
            gated_row(1, k, a1_scr)

    @pl.when(e == n_steps)
    def _():
        for k in range(rows_per_block):
            add_values(a1_scr, va_ref, k)


def _peer_main(xnt, u, v, tables, res, *, tt=512, eb=512):
    D, T = xnt.shape
    E = u.shape[0]
    r2, e2, cc, e1n = tables
    n_heads, n_keys, _ = r2.shape
    n_blocks = E // eb
    n_steps = n_blocks // 2
    tab = pl.BlockSpec((n_heads, n_keys, tt), lambda i, e: (0, 0, i))
    return pl.pallas_call(
        functools.partial(_peer_main_kernel, n_heads=n_heads, n_keys=n_keys, rows_per_block=eb // n_keys,
                          n_steps=n_steps),
        out_shape=jax.ShapeDtypeStruct((T, D), F32),
        grid=(T // tt, n_steps + 1),
        in_specs=[pl.BlockSpec((D, tt), lambda i, e: (0, i)),
                  pl.BlockSpec((2 * eb, D), lambda i, e: (jnp.minimum(e, n_steps - 1), 0)),
                  pl.BlockSpec((eb, D), lambda i, e: (jnp.maximum(2 * e - 1, 0), 0)),
                  pl.BlockSpec((eb, D), lambda i, e: (jnp.minimum(2 * e, n_blocks - 1), 0)),
                  tab, tab, tab, tab,
                  pl.BlockSpec((tt, D), lambda i, e: (i, 0))],
        out_specs=pl.BlockSpec((tt, D), lambda i, e: (i, 0)),
        scratch_shapes=[pltpu.VMEM((eb, tt), BF16), pltpu.VMEM((eb, tt), BF16), pltpu.VMEM((2, eb, tt), F32)],
        compiler_params=_cparams(("parallel", "arbitrary")),
        name="peer_main",
    )(xnt, u, v, v, r2, e2, cc, e1n, res)


def _peer_layer(x, gain, w_q, sub_keys, u, v):
    xnt = _rmsnorm_t(x, gain, BF16)
    tables = _peer_route(xnt, w_q.T.astype(BF16), sub_keys.astype(BF16))
    return _peer_main(xnt, u.astype(BF16), v.astype(BF16), tables, x)


def _filter_kernel(w1t_ref, w1c_ref, w1s_ref, b1_ref, fr1_ref, w2_ref, b2_ref, fr2_ref, w3_ref, two_ref, abs_ref,
                   *, L, D):
    i = pl.program_id(0)
    rt = two_ref.shape[0]
    hp = lax.Precision.HIGHEST
    rho = i * rt + lax.broadcasted_iota(jnp.int32, (rt, 1), 0)
    tau = jnp.where(rho < L, rho, 2 * L - rho).astype(F32)
    t = tau / float(L - 1)
    ang_base = (2.0 * math.pi) * tau / float(L)
    bidx = lax.broadcasted_iota(jnp.int32, (1, FILTER_BANDS), 1).astype(F32)
    bands = 1e-4 + bidx * ((FILTER_BANDS - 1 - 1e-4) / (FILTER_BANDS - 1))
    ang = ang_base * bands
    pre = (t * w1t_ref[...]
           + jnp.dot(jnp.cos(ang), w1c_ref[...], precision=hp, preferred_element_type=F32)
           - jnp.dot(jnp.sin(ang), w1s_ref[...], precision=hp, preferred_element_type=F32)
           + b1_ref[...])
    h = jnp.sin(fr1_ref[...] * pre)
    h = jnp.sin(fr2_ref[...] * (jnp.dot(h, w2_ref[...], precision=hp, preferred_element_type=F32) + b2_ref[...]))
    k = jnp.dot(h, w3_ref[...], precision=hp, preferred_element_type=F32)
    max_decay = math.log(DECAY_TARGET) / FAST_DECAY_PCT
    min_decay = math.log(DECAY_TARGET) / SLOW_DECAY_PCT
    didx = lax.broadcasted_iota(jnp.int32, (1, D), 1).astype(F32)
    deltas = jnp.abs(min_decay + didx * ((max_decay - min_decay) / (D - 1)))
    decay = jnp.exp(-t * deltas)
    fwd = jnp.where(rho < L, k[:, :D], 0.0)
    bwd = jnp.where((rho > L) | (rho == 0), k[:, D:], 0.0)
    two = (fwd + bwd) * decay
    two_ref[...] = two

    @pl.when(i == 0)
    def _():
        abs_ref[...] = jnp.zeros_like(abs_ref)

    abs_ref[...] += jnp.sum(jnp.abs(two), axis=0, keepdims=True)


def _hyena_filter(L, D, w1, b1, fr1, w2, b2, fr2, w3, *, rt=256):
    full = lambda a: pl.BlockSpec(a.shape, lambda i: (0,) * a.ndim)
    nb = FILTER_BANDS
    args = [w1[0:1], w1[1:1 + nb], w1[1 + nb:1 + 2 * nb], b1.reshape(1, -1), fr1.reshape(1, -1),
            w2, b2.reshape(1, -1), fr2.reshape(1, -1), w3]
    return pl.pallas_call(
        functools.partial(_filter_kernel, L=L, D=D),
        out_shape=(jax.ShapeDtypeStruct((2 * L, D), F32), jax.ShapeDtypeStruct((1, D), F32)),
        grid=(2 * L // rt,),
        in_specs=[full(a) for a in args],
        out_specs=(pl.BlockSpec((rt, D), lambda i: (i, 0)), pl.BlockSpec((1, D), lambda i: (0, 0))),
        compiler_params=_cparams(("arbitrary",)),
        name="hyena_filter",
    )(*args)


def _dft_constants(N, n_in_blocks):
    R, C = DFT_R, DFT_CHUNK
    A = N // R
    ka = np.arange(A)
    r0 = np.arange(C)
    a_in = np.arange(n_in_blocks)
    coef = np.exp(-2j * np.pi * (ka[:, None, None] * a_in[None, None, :] / A + r0[None, :, None] * ka[:, None, None] / N))
    m1 = np.zeros((2, A, C, n_in_blocks, C))
    for q in range(C):
        m1[0, :, q, :, q] = coef[:, q, :].real
        m1[1, :, q, :, q] = coef[:, q, :].imag
    m1 = m1.reshape(2 * A * C, n_in_blocks * C)
    icoef = np.conj(coef) / N
    m3 = np.zeros((n_in_blocks, C, 2, A, C))
    for q in range(C):
        m3[:, q, 0, :, q] = icoef[:, q, :].real.T
        m3[:, q, 1, :, q] = -icoef[:, q, :].imag.T
    m3 = m3.reshape(n_in_blocks * C, 2 * A * C)
    kr = np.arange(R)
    F = np.exp(-2j * np.pi * np.outer(kr, kr) / R)
    f2 = np.block([[F.real, -F.imag], [F.imag, F.real]])
    Fi = np.conj(F)
    i1 = np.block([[Fi.real, -Fi.imag], [Fi.imag, Fi.real]])
    r1 = np.arange(R // C)
    tw = np.exp(-2j * np.pi * C * np.outer(ka, r1) / N)
    return m1, f2, i1, m3, tw


def _twiddle(re, im, w):
    wr, wi = float(w.real), float(w.imag)
    if abs(wr - 1.0) < 1e-15 and abs(wi) < 1e-15:
        return re, im
    return re * wr - im * wi, re * wi + im * wr


def _forward_dft(xb_ref, n_in_blocks, m1_ref, f2_ref, tw, y_scr, emit):
    R, C = DFT_R, DFT_CHUNK
    A = tw.shape[0]
    m1 = m1_ref[...]
    for r1 in range(R // C):
        xin = jnp.concatenate([xb_ref[a * R + r1 * C: a * R + (r1 + 1) * C, :] for a in range(n_in_blocks)], axis=0)
        y = jnp.dot(m1, xin, preferred_element_type=F32)
        for k in range(A):
            re = y[k * C:(k + 1) * C, :]
            im = y[(A + k) * C:(A + k + 1) * C, :]
            re, im = _twiddle(re, im, tw[k, r1])
            y_scr[k, r1 * C:(r1 + 1) * C, :] = re.astype(BF16)
            y_scr[k, R + r1 * C:R + (r1 + 1) * C, :] = im.astype(BF16)
    f2 = f2_ref[...]
    for k in range(A):
        emit(k, jnp.dot(f2, y_scr[k], preferred_element_type=F32))


def _spectrum_kernel(two_ref, abs_ref, m1_ref, f2_ref, kf_ref, xb_scr, y_scr, *, tw):
    A = tw.shape[0]
    xb_scr[...] = two_ref[...].astype(BF16)
    inv = 1.0 / abs_ref[...]

    def emit(k, X):
        kf_ref[k] = X * inv

    _forward_dft(xb_scr, A, m1_ref, f2_ref, tw, y_scr, emit)


def _filter_spectrum(two, abssum, consts, *, dt=128):
    N, D = two.shape
    R = DFT_R
    A = N // R
    m1, f2, _, _, tw = consts
    m1 = jnp.asarray(m1, BF16)
    f2 = jnp.asarray(f2, BF16)
    return pl.pallas_call(
        functools.partial(_spectrum_kernel, tw=tw),
        out_shape=jax.ShapeDtypeStruct((A, 2 * R, D), F32),
        grid=(D // dt,),
        in_specs=[pl.BlockSpec((N, dt), lambda j: (0, j)),
                  pl.BlockSpec((1, dt), lambda j: (0, j)),
                  pl.BlockSpec(m1.shape, lambda j: (0, 0)),
                  pl.BlockSpec(f2.shape, lambda j: (0, 0))],
        out_specs=pl.BlockSpec((A, 2 * R, dt), lambda j: (0, 0, j)),
        scratch_shapes=[pltpu.VMEM((N, dt), BF16), pltpu.VMEM((A, 2 * R, dt), BF16)],
        compiler_params=_cparams(("parallel",)),
        name="hyena_filter_spectrum",
    )(two, abssum, m1, f2)


def _conv_kernel(u0_ref, u1_ref, u2_ref, s0_ref, s1_ref, s2_ref, skip_ref, kf_ref,
                 m1_ref, f2_ref, i1_ref, m3_ref, o_ref,
                 z_scr, zb_scr, x0_scr, y_scr, q_scr, *, L, tw):
    R, C = DFT_R, DFT_CHUNK
    A = tw.shape[0]
    nb = L // R
    dt = o_ref.shape[1]
    ridx = lax.broadcasted_iota(jnp.int32, (L, dt), 0)

    def short_conv(u_ref, s_ref):
        u = u_ref[...]
        prev = jnp.where(ridx == 0, 0.0, pltpu.roll(u, 1, axis=0))
        nxt = jnp.where(ridx == L - 1, 0.0, pltpu.roll(u, L - 1, axis=0))
        return prev * s_ref[0:1, :] + u * s_ref[1:2, :] + nxt * s_ref[2:3, :]

    x0_scr[...] = short_conv(u0_ref, s0_ref)
    z = short_conv(u2_ref, s2_ref) * short_conv(u1_ref, s1_ref)
    z_scr[...] = z
    zb_scr[...] = z.astype(BF16)

    i1 = i1_ref[...]

    def emit(k, X):
        xr, xi = X[:R, :], X[R:, :]
        kr, ki = kf_ref[k, 0:R, :], kf_ref[k, R:2 * R, :]
        p = jnp.concatenate([xr * kr - xi * ki, xr * ki + xi * kr], axis=0).astype(BF16)
        q = jnp.dot(i1, p, preferred_element_type=F32)
        for r1 in range(R // C):
            re = q[r1 * C:(r1 + 1) * C, :]
            im = q[R + r1 * C:R + (r1 + 1) * C, :]
            re, im = _twiddle(re, im, np.conj(tw[k, r1]))
            q_scr[r1, k * C:(k + 1) * C, :] = re.astype(BF16)
            q_scr[r1, (A + k) * C:(A + k + 1) * C, :] = im.astype(BF16)

    _forward_dft(zb_scr, nb, m1_ref, f2_ref, tw, y_scr, emit)

    m3 = m3_ref[...]
    skip = skip_ref[...]
    for r1 in range(R // C):
        o = jnp.dot(m3, q_scr[r1], preferred_element_type=F32)
        for a in range(nb):
            rows = slice(a * R + r1 * C, a * R + (r1 + 1) * C)
            y = (o[a * C:(a + 1) * C, :] + z_scr[rows, :] * skip) * x0_scr[rows, :]
            o_ref[rows, :] = y.astype(o_ref.dtype)


def _conv_lane_tile(L, D):
    per_lane_bytes = L * (3 * 4 * 2 + 4 + 2 + 4 + 2 * 4 * 2 + 4 * 4 * 2 + 2 * 2)
    for dt in (256, 128):
        if D % dt == 0 and per_lane_bytes * dt <= VMEM_LIMIT - 8 * 1024 * 1024:
            return dt
    return 128


def _hyena_conv(u, short, skip, kf, consts, *, row_start, batch, L):
    D = u.shape[1] // 3
    dt = _conv_lane_tile(L, D)
    R, C = DFT_R, DFT_CHUNK
    A = 2 * L // R
    m1, f2, i1, m3, tw = consts
    m1, f2, i1, m3 = (jnp.asarray(m, BF16) for m in (m1, f2, i1, m3))
    nd = D // dt
    b0 = row_start // L
    const = lambda m: pl.BlockSpec(m.shape, lambda j, b: (0, 0))
    return pl.pallas_call(
        functools.partial(_conv_kernel, L=L, tw=tw),
        out_shape=jax.ShapeDtypeStruct((batch * L, D), BF16),
        grid=(nd, batch),
        in_specs=[pl.BlockSpec((L, dt), lambda j, b: (b0 + b, j)),
                  pl.BlockSpec((L, dt), lambda j, b: (b0 + b, nd + j)),
                  pl.BlockSpec((L, dt), lambda j, b: (b0 + b, 2 * nd + j)),
                  pl.BlockSpec((3, dt), lambda j, b: (0, j)),
                  pl.BlockSpec((3, dt), lambda j, b: (0, nd + j)),
                  pl.BlockSpec((3, dt), lambda j, b: (0, 2 * nd + j)),
                  pl.BlockSpec((1, dt), lambda j, b: (0, j)),
                  pl.BlockSpec((A, 2 * R, dt), lambda j, b: (0, 0, j)),
                  const(m1), const(f2), const(i1), const(m3)],
        out_specs=pl.BlockSpec((L, dt), lambda j, b: (b, j)),
        scratch_shapes=[pltpu.VMEM((L, dt), F32), pltpu.VMEM((L, dt), BF16), pltpu.VMEM((L, dt), F32),
                        pltpu.VMEM((A, 2 * R, dt), BF16), pltpu.VMEM((R // C, 2 * A * C, dt), BF16)],
        compiler_params=_cparams(("parallel", "arbitrary")),
        name="hyena_conv",
    )(u, u, u, short, short, short, skip.reshape(1, D), kf, m1, f2, i1, m3)


def _hyena_layer(x, gain, seqs, w_in, b_in, w_short, f_w1, f_b1, f_fr1, f_w2, f_b2, f_fr2, f_w3, skip, w_out, b_out):
    T, D = x.shape
    h = _rmsnorm(x, gain, BF16)
    ones3 = jnp.ones((3 * D,), F32)
    u = _matmul(h, w_in.astype(BF16), b_in, ones3, None)
    groups = {}
    for start, length in seqs:
        groups.setdefault(length, []).append(start)
    ys = []
    for length, starts in groups.items():
        assert starts == list(range(starts[0], starts[0] + length * len(starts), length))
        two, abssum = _hyena_filter(length, D, f_w1, f_b1, f_fr1, f_w2, f_b2, f_fr2, f_w3)
        kf = _filter_spectrum(two, abssum, _dft_constants(2 * length, 2 * length // DFT_R))
        ys.append(_hyena_conv(u, w_short, skip, kf, _dft_constants(2 * length, length // DFT_R),
                              row_start=starts[0], batch=len(starts), L=length))
    y = jnp.concatenate(ys, axis=0) if len(ys) > 1 else ys[0]
    return _matmul(y, w_out.astype(BF16), b_out, jnp.ones((D,), F32), x)


def _pool_layer(x, gain, seqs, w, scale):
    D = x.shape[1]
    diff = _pool_diff(x, gain, seqs)
    return _matmul(diff, w.reshape(D, D).astype(BF16), jnp.zeros((D,), F32), scale, x)


def kernel(x_prompt, x_sample, mix_gain, ffn_gain, final_gain, pool_w, pool_scale, hy_w_in, hy_b_in, hy_short, hy_f_w1, hy_f_b1, hy_f_freq1, hy_f_w2, hy_f_b2, hy_f_freq2, hy_f_w3, hy_skip, hy_w_out, hy_b_out, peer_w_q, peer_sub_keys, peer_u, peer_v):
    D = x_prompt.shape[-1]
    depth = mix_gain.shape[0]
    seqs = []
    row = 0
    for xs in (x_prompt, x_sample):
        for _ in range(xs.shape[0]):
            seqs.append((row, xs.shape[1]))
            row += xs.shape[1]
    seqs = tuple(seqs)
    x = jnp.concatenate([x_prompt.reshape(-1, D), x_sample.reshape(-1, D)], axis=0)
    for i in range(depth):
        j = i // 2
        if i % 2 == 0:
            x = _pool_layer(x, mix_gain[i], seqs, pool_w[j], pool_scale[j])
        else:
            x = _hyena_layer(x, mix_gain[i], seqs, hy_w_in[j], hy_b_in[j], hy_short[j], hy_f_w1[j], hy_f_b1[j],
                             hy_f_freq1[j], hy_f_w2[j], hy_f_b2[j], hy_f_freq2[j], hy_f_w3[j], hy_skip[j],
                             hy_w_out[j], hy_b_out[j])
        x = _peer_layer(x, ffn_gain[i], peer_w_q[i], peer_sub_keys[i], peer_u[i], peer_v[i])
    n_p = x_prompt.shape[0] * x_prompt.shape[1]
    n_s = x_sample.shape[0] * x_sample.shape[1]
    y_p = _rmsnorm(x, final_gain, F32, row_start=0, n_rows=n_p)
    y_s = _rmsnorm(x, final_gain, F32, row_start=n_p, n_rows=n_s)
    return (y_p.reshape(x_prompt.shape), y_s.reshape(x_sample.shape))
```
